```python
import jax, jax.numpy as jnp
from jax import lax
import numpy as np

D_MODEL = 1024
BATCH = 8
SEQ = 4096
DEPTH = 2

CHUNK = 64
ROPE_THETA = 500000.0
DN_ALPHA = (2 * DEPTH) ** 0.25
DN_BETA = (8 * DEPTH) ** -0.25
LN_EPS = 1e-5
RMS_EPS = 1e-6

A_HEADS = 8
A_HEAD_DIM = 64
A_WIDTH = A_HEADS * A_HEAD_DIM
A_DECAY_RANK = 64
A_ICLR_RANK = 64
A_GN_EPS = 64e-5

B_HEADS = 8
B_HEAD_DIM = 64
B_WIDTH = B_HEADS * B_HEAD_DIM
B_KV_RANK = 128
IDX_HEADS = 8
IDX_DIM = 32
TOPK_MAX = 256
Q_BLOCK = 128

C_HEADS = 4
C_KEY_DIM = 128
C_VAL_DIM = 256
C_KEY_WIDTH = C_HEADS * C_KEY_DIM
C_VAL_WIDTH = C_HEADS * C_VAL_DIM
C_GATE_RANK = 16
C_GATE_NORM = 16.0

A_SHIFT_SIZES = [A_WIDTH, A_WIDTH, A_WIDTH, A_DECAY_RANK, A_ICLR_RANK]
EVEN_REST_SIZES = [A_WIDTH, B_WIDTH, B_KV_RANK, IDX_HEADS * IDX_DIM, IDX_DIM, IDX_HEADS, B_WIDTH]
EVEN_SHIFT = sum(A_SHIFT_SIZES)
EVEN_IN = EVEN_SHIFT + sum(EVEN_REST_SIZES)
EVEN_OUT_IN = A_WIDTH + B_WIDTH
ODD_SIZES = [C_KEY_WIDTH, C_KEY_WIDTH, C_VAL_WIDTH, C_GATE_RANK, C_VAL_WIDTH]
ODD_IN = sum(ODD_SIZES)
N_EVEN = (DEPTH + 1) // 2
N_ODD = DEPTH // 2

kernel_name = 'hybrid_rwkv7_dsa_gla_deepnorm'

F32 = jnp.float32


def split_cols(p, sizes):
    idx = [int(i) for i in np.cumsum(sizes)[:-1]]
    return jnp.split(p, idx, axis=-1)


def layer_norm(x, g, b):
    xf = x.astype(F32)
    mu = jnp.mean(xf, -1, keepdims=True)
    var = jnp.mean(jnp.square(xf - mu), -1, keepdims=True)
    return ((xf - mu) * lax.rsqrt(var + LN_EPS) * g + b).astype(x.dtype)


def rms_norm(x, g):
    xf = x.astype(F32)
    return xf * lax.rsqrt(jnp.mean(jnp.square(xf), -1, keepdims=True) + RMS_EPS) * g


def token_shift(p):
    return jnp.pad(p, ((0, 0), (1, 0), (0, 0)))[:, :-1, :]


def partial_rope(x, pos):
    hd = x.shape[-1]
    rd = hd // 4
    half = rd // 2
    inv = jnp.power(ROPE_THETA, -jnp.arange(half, dtype=F32) * 2.0 / rd)
    ang = pos.astype(F32)[..., None] * inv
    ang = ang.reshape(ang.shape[:2] + (1,) * (x.ndim - 3) + (half,))
    cos, sin = jnp.cos(ang), jnp.sin(ang)
    xf = x.astype(F32)
    x1, x2, xp = xf[..., :half], xf[..., half:rd], xf[..., rd:]
    return jnp.concatenate([x1 * cos - x2 * sin, x2 * cos + x1 * sin, xp], -1)


def rwkv7_mix(r, k, v, wd, ad, w0, w2, a0, a2, kk_scale, ka, rk, gn_g, gn_b):
    Bn, T, _ = r.shape
    H, Dh = A_HEADS, A_HEAD_DIM
    r, k, v, wd, ad = (t.astype(F32) for t in (r, k, v, wd, ad))
    w_log = -jax.nn.softplus(-(w0 + jnp.tanh(wd) @ w2)) - 0.5
    decay = jnp.exp(-jnp.exp(w_log))
    a = jax.nn.sigmoid(a0 + ad @ a2)
    kk = (k * kk_scale).reshape(Bn, T, H, Dh)
    kk = kk / jnp.maximum(jnp.sqrt(jnp.sum(jnp.square(kk), -1, keepdims=True)), 1e-12)
    k = k * (1.0 + (a - 1.0) * ka)
    hd = lambda t: t.reshape(Bn, T, H, Dh)
    r_h, k_h, v_h, w_h, a_h = hd(r), hd(k), hd(v), hd(decay), hd(a)
    b_h = kk * a_h

    def step(S, inp):
        r_t, w_t, k_t, v_t, kk_t, b_t = inp
        sa = jnp.einsum('bhvk,bhk->bhv', S, -kk_t)
        S = S * w_t[:, :, None, :] + sa[..., None] * b_t[:, :, None, :] + v_t[..., None] * k_t[:, :, None, :]
        return S, jnp.einsum('bhvk,bhk->bhv', S, r_t)

    xs = tuple(jnp.moveaxis(t, 1, 0) for t in (r_h, w_h, k_h, v_h, kk, b_h))
    _, y = lax.scan(step, jnp.zeros((Bn, H, Dh, Dh), F32), xs)
    y = jnp.moveaxis(y, 0, 1)
    mu = jnp.mean(y, -1, keepdims=True)
    var = jnp.mean(jnp.square(y - mu), -1, keepdims=True)
    y = (y - mu) * lax.rsqrt(var + A_GN_EPS) * gn_g.reshape(H, Dh) + gn_b.reshape(H, Dh)
    bonus = jnp.sum(r_h * k_h * rk, -1, keepdims=True) * v_h
    return (y + bonus).reshape(Bn, T, A_WIDTH)


def dsa_mix(q, ckv, qi, ki, wi, pos, kvn_g, wuk, wuv):
    Bn, T, _ = q.shape
    c = rms_norm(ckv, kvn_g)
    k = partial_rope(c @ wuk, pos)
    v = c @ wuv
    q = partial_rope(q.reshape(Bn, T, B_HEADS, B_HEAD_DIM), pos)
    qi = partial_rope(qi.reshape(Bn, T, IDX_HEADS, IDX_DIM), pos)
    ki = partial_rope(ki, pos)
    wi = wi.astype(F32) * (IDX_HEADS ** -0.5 * IDX_DIM ** -0.5)
    top_k = min(TOPK_MAX, T // 4)
    nblk = T // Q_BLOCK
    key_chunk = jnp.arange(T) // CHUNK

    def to_blocks(t):
        return jnp.moveaxis(t.reshape((Bn, nblk, Q_BLOCK) + t.shape[2:]), 1, 0)

    def block(args):
        qb, qib, wib, s0 = args
        q_chunk = (s0 + jnp.arange(Q_BLOCK)) // CHUNK
        adm = key_chunk[None, :] <= q_chunk[:, None]
        score = jnp.einsum('bqh,bqhs->bqs', wib, jax.nn.relu(jnp.einsum('bqhd,bsd->bqhs', qib, ki)))
        score = jnp.where(adm[None], score, -jnp.inf)
        vals, idx = lax.top_k(score, top_k)
        valid = vals > -jnp.inf
        k_sel = jax.vmap(lambda kb, ib: kb[ib])(k, idx)
        v_sel = jax.vmap(lambda vb, ib: vb[ib])(v, idx)
        s = jnp.einsum('bqhd,bqkd->bqhk', qb, k_sel) * (B_HEAD_DIM ** -0.5)
        s = jnp.where(valid[:, :, None, :], s, -1e30)
        p = jax.nn.softmax(s, axis=-1)
        return jnp.einsum('bqhk,bqkd->bqhd', p, v_sel)

    starts = jnp.arange(nblk, dtype=jnp.int32) * Q_BLOCK
    out = lax.map(block, (to_blocks(q), to_blocks(qi), to_blocks(wi), starts))
    return jnp.moveaxis(out, 0, 1).reshape(Bn, T, B_WIDTH)


def gla_mix(q, k, v, gd, g2, g_bias, norm_g):
    Bn, T, _ = q.shape
    H, Dk, Dv = C_HEADS, C_KEY_DIM, C_VAL_DIM
    n = T // CHUNK
    log_a = jax.nn.log_sigmoid((gd @ g2 + g_bias).astype(F32)) / C_GATE_NORM
    chunks = lambda t, d: t.astype(F32).reshape(Bn, n, CHUNK, H, d)
    qc = chunks(q, Dk) * (Dk ** -0.5)
    kc, vc, gc = chunks(k, Dk), chunks(v, Dv), chunks(log_a, Dk)
    b = jnp.cumsum(gc, axis=2)
    q_e = qc * jnp.exp(b)
    k_e = kc * jnp.exp(-b)
    causal = jnp.tril(jnp.ones((CHUNK, CHUNK), bool))
    att = jnp.where(causal, jnp.einsum('bnthd,bnshd->bnhts', q_e, k_e), 0.0)
    o_intra = jnp.einsum('bnhts,bnshd->bnthd', att, vc)
    b_last = b[:, :, -1]
    chunk_kv = jnp.einsum('bnshk,bnshv->bnhkv', kc * jnp.exp(b_last[:, :, None] - b), vc)

    def step(S, inp):
        dec, kv = inp
        return S * dec[..., None] + kv, S

    _, S_prev = lax.scan(step, jnp.zeros((Bn, H, Dk, Dv), F32),
                         (jnp.moveaxis(jnp.exp(b_last), 1, 0), jnp.moveaxis(chunk_kv, 1, 0)))
    S_prev = jnp.moveaxis(S_prev, 0, 1)
    o = o_intra + jnp.einsum('bnthk,bnhkv->bnthv', q_e, S_prev)
    return rms_norm(o, norm_g).reshape(Bn, T, C_VAL_WIDTH)


def setup_inputs(seed: int = 0) -> dict:
    key = jax.random.key(seed)
    ks = jax.random.split(key, 24)
    nrm = lambda k, s, sc: jax.random.normal(k, s, F32) * sc
    offs = jax.random.randint(ks[1], (BATCH, 1), 0, 64, dtype=jnp.int32) * CHUNK
    positions = (offs + jnp.arange(SEQ, dtype=jnp.int32)[None, :]).astype(jnp.int32)
    return {
        'x': jax.random.normal(ks[0], (BATCH, SEQ, D_MODEL), F32),
        'positions': positions,
        'e_w_in': nrm(ks[2], (N_EVEN, D_MODEL, EVEN_IN), D_MODEL ** -0.5),
        'a_mu': jax.random.uniform(ks[3], (N_EVEN, EVEN_SHIFT), F32),
        'a_w0': -1.0 + nrm(ks[4], (N_EVEN, A_WIDTH), 0.5),
        'a_w2': nrm(ks[5], (N_EVEN, A_DECAY_RANK, A_WIDTH), 0.5 * A_DECAY_RANK ** -0.5),
        'a_a0': nrm(ks[6], (N_EVEN, A_WIDTH), 0.1),
        'a_a2': nrm(ks[7], (N_EVEN, A_ICLR_RANK, A_WIDTH), 0.5 * A_ICLR_RANK ** -0.5),
        'a_kk_scale': 0.85 + nrm(ks[8], (N_EVEN, A_WIDTH), 0.02),
        'a_ka': 1.0 + nrm(ks[9], (N_EVEN, A_WIDTH), 0.02),
        'a_rk': nrm(ks[10], (N_EVEN, A_HEADS, A_HEAD_DIM), 0.1),
        'a_gn_g': 1.0 + nrm(ks[11], (N_EVEN, A_WIDTH), 0.02),
        'a_gn_b': nrm(ks[12], (N_EVEN, A_WIDTH), 0.02),
        'b_kv_norm_g': 1.0 + nrm(ks[13], (N_EVEN, B_KV_RANK), 0.02),
        'b_wuk': nrm(ks[14], (N_EVEN, B_KV_RANK, B_HEAD_DIM), B_KV_RANK ** -0.5),
        'b_wuv': nrm(ks[15], (N_EVEN, B_KV_RANK, B_HEAD_DIM), B_KV_RANK ** -0.5),
        'e_w_out': nrm(ks[16], (N_EVEN, EVEN_OUT_IN, D_MODEL), DN_BETA * EVEN_OUT_IN ** -0.5),
        'o_w_in': nrm(ks[17], (N_ODD, D_MODEL, ODD_IN), D_MODEL ** -0.5),
        'c_g2': nrm(ks[18], (N_ODD, C_GATE_RANK, C_KEY_WIDTH), C_GATE_RANK ** -0.5),
        'c_g_bias': 1.0 + nrm(ks[19], (N_ODD, C_KEY_WIDTH), 0.1),
        'c_norm_g': 1.0 + nrm(ks[20], (N_ODD, C_VAL_DIM), 0.02),
        'o_w_out': nrm(ks[21], (N_ODD, C_VAL_WIDTH, D_MODEL), DN_BETA * C_VAL_WIDTH ** -0.5),
        'ln_g': 1.0 + nrm(ks[22], (DEPTH, D_MODEL), 0.02),
        'ln_b': nrm(ks[23], (DEPTH, D_MODEL), 0.02),
    }


def reference(x, positions, e_w_in, a_mu, a_w0, a_w2, a_a0, a_a2, a_kk_scale, a_ka, a_rk, a_gn_g, a_gn_b,
              b_kv_norm_g, b_wuk, b_wuv, e_w_out, o_w_in, c_g2, c_g_bias, c_norm_g, o_w_out, ln_g, ln_b):
    h = x
    for i in range(DEPTH):
        j = i // 2
        if i % 2 == 0:
            p = h @ e_w_in[j]
            ps, pr = p[..., :EVEN_SHIFT], p[..., EVEN_SHIFT:]
            ps = ps + (token_shift(ps) - ps) * a_mu[j]
            r, k, v, wd, ad = split_cols(ps, A_SHIFT_SIZES)
            g_a, q_b, ckv, qi, ki, wi, g_b = split_cols(pr, EVEN_REST_SIZES)
            ya = rwkv7_mix(r, k, v, wd, ad, a_w0[j], a_w2[j], a_a0[j], a_a2[j], a_kk_scale[j], a_ka[j],
                           a_rk[j], a_gn_g[j], a_gn_b[j]) * jax.nn.silu(g_a.astype(F32))
            yb = dsa_mix(q_b, ckv, qi, ki, wi, positions, b_kv_norm_g[j], b_wuk[j], b_wuv[j]) * jax.nn.silu(g_b.astype(F32))
            y = jnp.concatenate([ya, yb], -1).astype(h.dtype) @ e_w_out[j]
        else:
            p = h @ o_w_in[j]
            q_c, k_c, v_c, gd, g_c = split_cols(p, ODD_SIZES)
            yc = gla_mix(q_c, k_c, v_c, gd, c_g2[j], c_g_bias[j], c_norm_g[j]) * jax.nn.silu(g_c.astype(F32))
            y = yc.astype(h.dtype) @ o_w_out[j]
        h = layer_norm(DN_ALPHA * h + y.astype(h.dtype), ln_g[i], ln_b[i])
    return h
```

```python
import functools

import jax
import jax.numpy as jnp
import numpy as np
from jax import lax
from jax.experimental import pallas as pl
from jax.experimental.pallas import tpu as pltpu

F32 = jnp.float32
BF16 = jnp.bfloat16

D_MODEL = 1024
DEPTH = 2
CHUNK = 64
ROPE_THETA = 500000.0
DN_ALPHA = (2 * DEPTH) ** 0.25
LN_EPS = 1e-5
RMS_EPS = 1e-6

A_HEADS = 8
A_HEAD_DIM = 64
A_WIDTH = A_HEADS * A_HEAD_DIM
A_RANK = 64
A_GN_EPS = 64e-5

B_HEADS = 8
B_HEAD_DIM = 64
B_WIDTH = B_HEADS * B_HEAD_DIM
B_KV_RANK = 128
IDX_HEADS = 8
IDX_DIM = 32
TOPK_MAX = 256
Q_BLOCK = 128

C_HEADS = 4
C_KEY_DIM = 128
C_VAL_DIM = 256
C_KEY_WIDTH = C_HEADS * C_KEY_DIM
C_VAL_WIDTH = C_HEADS * C_VAL_DIM
C_GATE_RANK = 16
C_GATE_NORM = 16.0

LANES = 128
VMEM_LIMIT = 56 * 1024 * 1024
NEG_BIG = -1e30
INT_MIN = -2147483648
KEY_NEG_INF = -2139095041


def _cparams(n_axes):
    return pltpu.CompilerParams(dimension_semantics=("arbitrary",) * n_axes,
                                vmem_limit_bytes=VMEM_LIMIT)


def _hi_lo(x):
    hi = x.astype(BF16)
    lo = (x - hi.astype(F32)).astype(BF16)
    return hi, lo


def _dot(a, b):
    return jnp.dot(a.astype(BF16), b.astype(BF16), preferred_element_type=F32)


def _dot_nt(a, b):
    return lax.dot_general(a.astype(BF16), b.astype(BF16), (((1,), (1,)), ((), ())),
                           preferred_element_type=F32)


def _dot_tn(a, b):
    return lax.dot_general(a.astype(BF16), b.astype(BF16), (((0,), (0,)), ((), ())),
                           preferred_element_type=F32)


def _dotx(a, b):
    ah, al = _hi_lo(a)
    bh, bl = _hi_lo(b)
    d = functools.partial(jnp.dot, preferred_element_type=F32)
    return d(ah, bh) + d(ah, bl) + d(al, bh)


def _dotx_exact_lhs(a_bf16, b):
    bh, bl = _hi_lo(b)
    bl2 = (b - bh.astype(F32) - bl.astype(F32)).astype(BF16)
    d = functools.partial(jnp.dot, preferred_element_type=F32)
    return d(a_bf16, bh) + d(a_bf16, bl) + d(a_bf16, bl2)


def _sigmoid(x):
    return 1.0 / (1.0 + jnp.exp(-x))


def _silu(x):
    return x * _sigmoid(x)


def _softplus(x):
    return jnp.maximum(x, 0.0) + jnp.log(1.0 + jnp.exp(-jnp.abs(x)))


def _proj_cols(xb, w_ref, o_ref, step=512):
    n = o_ref.shape[1]
    for n0 in range(0, n, step):
        n1 = min(n, n0 + step)
        o_ref[:, n0:n1] = jnp.dot(xb, w_ref[:, n0:n1], preferred_element_type=F32)


def _proj_kernel(x_ref, w_ref, o_ref):
    _proj_cols(x_ref[...].astype(BF16), w_ref, o_ref)


def _proj_idx_kernel(x_ref, w_ref, wh_ref, wl_ref, o_ref, oi_ref):
    x = x_ref[...]
    xh, xl = _hi_lo(x)
    _proj_cols(xh, w_ref, o_ref)
    d = functools.partial(jnp.dot, preferred_element_type=F32)
    oi_ref[...] = d(xh, wh_ref[...]) + d(xl, wh_ref[...]) + d(xh, wl_ref[...])


def _project(x2d, w, tm=256):
    m, k = x2d.shape
    n = w.shape[1]
    return pl.pallas_call(
        _proj_kernel,
        grid=(m // tm,),
        in_specs=[pl.BlockSpec((tm, k), lambda i: (i, 0)),
                  pl.BlockSpec((k, n), lambda i: (0, 0))],
        out_specs=pl.BlockSpec((tm, n), lambda i: (i, 0)),
        out_shape=jax.ShapeDtypeStruct((m, n), F32),
        compiler_params=_cparams(1),
        name="proj_odd",
    )(x2d, w)


def _project_with_idx(x2d, w, wih, wil, tm=256):
    m, k = x2d.shape
    n = w.shape[1]
    ni = wih.shape[1]
    return pl.pallas_call(
        _proj_idx_kernel,
        grid=(m // tm,),
        in_specs=[pl.BlockSpec((tm, k), lambda i: (i, 0)),
                  pl.BlockSpec((k, n), lambda i: (0, 0)),
                  pl.BlockSpec((k, ni), lambda i: (0, 0)),
                  pl.BlockSpec((k, ni), lambda i: (0, 0))],
        out_specs=[pl.BlockSpec((tm, n), lambda i: (i, 0)),
                   pl.BlockSpec((tm, ni), lambda i: (i, 0))],
        out_shape=[jax.ShapeDtypeStruct((m, n), F32),
                   jax.ShapeDtypeStruct((m, ni), F32)],
        compiler_params=_cparams(1),
        name="proj_even",
    )(x2d, w, wih, wil)


_PA_MU_R, _PA_MU_K, _PA_MU_V, _PA_W0, _PA_A0, _PA_KKS, _PA_KA, _PA_RK, _PA_GNG, _PA_GNB = range(10)


def _rwkv_kernel(r_ref, k_ref, v_ref, wa_ref, ga_ref, pa_ref, muwa_ref, w2_ref, a2_ref, bd_ref,
                 o_ref, pr_ref, pk_ref, pv_ref, pwa_ref, st_ref):
    c = pl.program_id(1)
    L = r_ref.shape[0]
    H, Dh = A_HEADS, A_HEAD_DIM

    @pl.when(c == 0)
    def _():
        pr_ref[...] = jnp.zeros_like(pr_ref)
        pk_ref[...] = jnp.zeros_like(pk_ref)
        pv_ref[...] = jnp.zeros_like(pv_ref)
        pwa_ref[...] = jnp.zeros_like(pwa_ref)
        st_ref[...] = jnp.zeros_like(st_ref)

    row = lax.broadcasted_iota(jnp.int32, (L, 1), 0)

    def shift_mix(x_ref, prev_ref, mu):
        x = x_ref[...]
        xs = jnp.where(row == 0, prev_ref[0:1, :], pltpu.roll(x, 1, 0))
        prev_ref[0:1, :] = x[L - 1:L, :]
        return x + (xs - x) * mu

    pa = pa_ref[...]
    prow = lambda i: pa[i:i + 1, :]
    r = shift_mix(r_ref, pr_ref, prow(_PA_MU_R))
    k = shift_mix(k_ref, pk_ref, prow(_PA_MU_K))
    v = shift_mix(v_ref, pv_ref, prow(_PA_MU_V))
    wa = shift_mix(wa_ref, pwa_ref, muwa_ref[...])

    lane = lax.broadcasted_iota(jnp.int32, (1, LANES), 1)
    twa = jnp.where(lane < A_RANK, jnp.tanh(wa), wa)
    z_w = prow(_PA_W0) + _dotx(twa, w2_ref[...])
    z_a = prow(_PA_A0) + _dotx(twa, a2_ref[...])
    logw = -jnp.exp(-_softplus(-z_w) - 0.5)
    a = _sigmoid(z_a)
    kk = k * prow(_PA_KKS)
    bd = bd_ref[...]
    kk2 = kk * kk
    k2h, k2l = _hi_lo(kk2)
    n2 = (jnp.dot(k2h, bd, preferred_element_type=F32)
          + jnp.dot(k2l, bd, preferred_element_type=F32))
    kk = kk / jnp.maximum(jnp.sqrt(n2), 1e-12)
    kmod = k * (1.0 + (a - 1.0) * prow(_PA_KA))
    bvec = kk * a

    ti = lax.broadcasted_iota(jnp.int32, (L, L), 0)
    si = lax.broadcasted_iota(jnp.int32, (L, L), 1)
    incl = si <= ti
    strict = si < ti
    cw = _dotx_exact_lhs(incl.astype(BF16), logw)
    cw_last = cw[L - 1:L, :]
    cref = cw[L // 2 - 1:L // 2, :]
    winv_ref = jnp.exp(-cref)
    r_abs = r * jnp.exp(cw)
    a_abs = -kk * jnp.exp(cw - logw)
    r_rel = r_abs * winv_ref
    a_rel = a_abs * winv_ref
    eneg = jnp.exp(cref - cw)
    b_rel = bvec * eneg
    k_rel = kmod * eneg
    elast = jnp.exp(cw_last - cw)
    b_hat = bvec * elast
    k_hat = kmod * elast
    wl = jnp.exp(cw_last)
    bonus_src = r * kmod * prow(_PA_RK)

    eye = ti == si
    eye_d = (lax.broadcasted_iota(jnp.int32, (Dh, Dh), 0)
             == lax.broadcasted_iota(jnp.int32, (Dh, Dh), 1))
    n_sq = int(np.log2(L)) - 1
    outs = []
    for h in range(H):
        sl = slice(h * Dh, (h + 1) * Dh)
        a_rel_h, r_rel_h, b_rel_h, k_rel_h = a_rel[:, sl], r_rel[:, sl], b_rel[:, sl], k_rel[:, sl]
        v_h = v[:, sl]
        A_ab = jnp.where(strict, _dot_nt(a_rel_h, b_rel_h), 0.0)
        A_ak = jnp.where(strict, _dot_nt(a_rel_h, k_rel_h), 0.0)
        A_rb = jnp.where(incl, _dot_nt(r_rel_h, b_rel_h), 0.0)
        A_rk = jnp.where(incl, _dot_nt(r_rel_h, k_rel_h), 0.0)
        X = A_ab
        Tm = jnp.where(eye, 1.0, 0.0) + X
        for _ in range(n_sq):
            X = _dot(X, X)
            Tm = Tm + _dot(Tm, X)
        P1 = _dot(Tm, a_abs[:, sl])
        P2 = _dot(Tm, _dot(A_ak, v_h))
        Q1 = r_abs[:, sl] + _dot(A_rb, P1)
        Q2 = _dot(A_rb, P2) + _dot(A_rk, v_h)
        b_hat_h = b_hat[:, sl]
        Mx = jnp.where(eye_d, wl[:, sl], 0.0) + _dot_tn(b_hat_h, P1)
        Nx = _dot_tn(b_hat_h, P2) + _dot_tn(k_hat[:, sl], v_h)
        S = st_ref[h]
        y = _dotx(Q1, S) + Q2
        st_ref[h] = _dotx(Mx, S) + Nx
        mu = jnp.mean(y, axis=-1, keepdims=True)
        yc = y - mu
        var = jnp.mean(yc * yc, axis=-1, keepdims=True)
        yn = yc * lax.rsqrt(var + A_GN_EPS)
        bonus = jnp.sum(bonus_src[:, sl], axis=-1, keepdims=True) * v_h
        outs.append((yn, bonus))
    yn = jnp.concatenate([o[0] for o in outs], axis=1)
    bonus = jnp.concatenate([o[1] for o in outs], axis=1)
    ya = yn * prow(_PA_GNG) + prow(_PA_GNB) + bonus
    o_ref[...] = ya * _silu(ga_ref[...])


def _rwkv(p_main, cols, B, T, pa, muwa, w2p, a2p, bd, L=CHUNK):
    nc = T // L
    m = B * T
    blk = lambda w, ci: pl.BlockSpec((L, w), lambda b, c, ci=ci: (b * nc + c, ci))
    full = lambda arr: pl.BlockSpec(arr.shape, lambda b, c: (0,) * arr.ndim)
    return pl.pallas_call(
        _rwkv_kernel,
        grid=(B, nc),
        in_specs=[blk(A_WIDTH, cols["r"]), blk(A_WIDTH, cols["k"]), blk(A_WIDTH, cols["v"]),
                  blk(LANES, cols["wa"]), blk(A_WIDTH, cols["ga"]),
                  full(pa), full(muwa), full(w2p), full(a2p), full(bd)],
        out_specs=pl.BlockSpec((L, A_WIDTH), lambda b, c: (b * nc + c, 0)),
        out_shape=jax.ShapeDtypeStruct((m, A_WIDTH), F32),
        scratch_shapes=[pltpu.VMEM((8, A_WIDTH), F32), pltpu.VMEM((8, A_WIDTH), F32),
                        pltpu.VMEM((8, A_WIDTH), F32), pltpu.VMEM((8, LANES), F32),
                        pltpu.VMEM((A_HEADS, A_HEAD_DIM, A_HEAD_DIM), F32)],
        compiler_params=_cparams(2),
        name="rwkv7",
    )(p_main, p_main, p_main, p_main, p_main, pa, muwa, w2p, a2p, bd)


def _rope(x, c, sa, sb, shift):
    n = x.shape[-1]
    return x * c + pltpu.roll(x, n - shift, 1) * sa + pltpu.roll(x, shift, 1) * sb


def _tile_lanes(t, n):
    return t if n == 1 else jnp.concatenate([t] * n, axis=1)


def _dsa_prep_kernel(q_ref, ckv_ref, qi_ref, wi_ref, ki_ref, c64_ref, sa64_ref, sb64_ref,
                     c32_ref, sa32_ref, sb32_ref, g_ref, wkv_ref,
                     qr_ref, kv_ref, vt_ref, qiw_ref, ki3_ref):
    tm = q_ref.shape[0]
    c64, sa64, sb64 = c64_ref[...], sa64_ref[...], sb64_ref[...]
    c32, sa32, sb32 = c32_ref[...], sa32_ref[...], sb32_ref[...]
    nq = B_WIDTH // LANES
    q = q_ref[...] * (B_HEAD_DIM ** -0.5)
    qr_ref[...] = _rope(q, _tile_lanes(c64, nq), _tile_lanes(sa64, nq), _tile_lanes(sb64, nq), 8)

    ckv = ckv_ref[...]
    cn = ckv * lax.rsqrt(jnp.mean(ckv * ckv, axis=-1, keepdims=True) + RMS_EPS) * g_ref[...]
    kv = jnp.dot(cn.astype(BF16), wkv_ref[...], preferred_element_type=F32)
    lane = lax.broadcasted_iota(jnp.int32, (1, LANES), 1)
    is_k = lane < B_HEAD_DIM
    kv = _rope(kv, jnp.where(is_k, c64, 1.0), jnp.where(is_k, sa64, 0.0), jnp.where(is_k, sb64, 0.0), 8)
    kv_ref[...] = kv.astype(BF16)
    vt = kv.T[B_HEAD_DIM:, :]
    for i in range(tm // LANES):
        vt_ref[0, i] = vt[:, i * LANES:(i + 1) * LANES].astype(BF16)

    nqi = (IDX_HEADS * IDX_DIM) // LANES
    qiw_ref[:, :IDX_HEADS * IDX_DIM] = _rope(qi_ref[...], _tile_lanes(c32, nqi), _tile_lanes(sa32, nqi),
                                              _tile_lanes(sb32, nqi), 4)
    qiw_ref[:, IDX_HEADS * IDX_DIM:] = wi_ref[...] * (IDX_HEADS ** -0.5 * IDX_DIM ** -0.5)

    kir = _rope(ki_ref[...], c32, sa32, sb32, 4)
    kh = kir.astype(BF16)
    kl = (kir - kh.astype(F32)).astype(BF16)
    ki3 = jnp.where(lane < 2 * IDX_DIM, kh, jnp.where(lane < 3 * IDX_DIM, kl, jnp.zeros_like(kl)))
    ki3_ref[...] = ki3


def _dsa_prep(p_main, p_idx, cols, tabs, g, wkv, B, T, tm=256):
    m = B * T
    nb = m // tm
    blk = lambda w, ci: pl.BlockSpec((tm, w), lambda i, ci=ci: (i, ci))
    full = lambda arr: pl.BlockSpec(arr.shape, lambda i: (0,) * arr.ndim)
    tspec = pl.BlockSpec((tm, LANES), lambda i: (i, 0))
    sub = tm // LANES
    return pl.pallas_call(
        _dsa_prep_kernel,
        grid=(nb,),
        in_specs=[blk(B_WIDTH, cols["qb"]), blk(LANES, cols["ckv"]),
                  blk(2 * LANES, 0), blk(LANES, 2), blk(LANES, 3),
                  tspec, tspec, tspec, tspec, tspec, tspec, full(g), full(wkv)],
        out_specs=[pl.BlockSpec((tm, B_WIDTH), lambda i: (i, 0)),
                   pl.BlockSpec((tm, LANES), lambda i: (i, 0)),
                   pl.BlockSpec((1, sub, B_HEAD_DIM, LANES), lambda i: (i, 0, 0, 0)),
                   pl.BlockSpec((tm, 3 * LANES), lambda i: (i, 0)),
                   pl.BlockSpec((tm, LANES), lambda i: (i, 0))],
        out_shape=[jax.ShapeDtypeStruct((m, B_WIDTH), F32),
                   jax.ShapeDtypeStruct((m, LANES), BF16),
                   jax.ShapeDtypeStruct((nb, sub, B_HEAD_DIM, LANES), BF16),
                   jax.ShapeDtypeStruct((m, 3 * LANES), F32),
                   jax.ShapeDtypeStruct((m, LANES), BF16)],
        compiler_params=_cparams(1),
        name="dsa_prep",
    )(p_main, p_main, p_idx, p_idx, p_idx, *tabs, g, wkv)


def _dsa_kernel(top_k, qr_ref, qiw_ref, gb_ref, kv_ref, vt_ref, ki3_ref, o_ref,
                qt_scr, rhs_scr, key_scr, m_scr, l_scr, acc_scr):
    j = pl.program_id(1)
    nk = j + 1
    QB = Q_BLOCK
    H, Dh = B_HEADS, B_HEAD_DIM

    qT = qr_ref[...].T
    zq = jnp.zeros((QB - Dh, QB), BF16)
    for h in range(H):
        qt_scr[h] = jnp.concatenate([qT[h * Dh:(h + 1) * Dh, :].astype(BF16), zq], axis=0)
    qiwT = qiw_ref[...].T
    qiT = qiwT[:IDX_HEADS * IDX_DIM, :]
    wiT = qiwT[IDX_HEADS * IDX_DIM:IDX_HEADS * IDX_DIM + IDX_HEADS, :]
    qh, ql = _hi_lo(qiT)
    zi = jnp.zeros((LANES - 3 * IDX_DIM, QB), BF16)
    for h in range(IDX_HEADS):
        s = slice(h * IDX_DIM, (h + 1) * IDX_DIM)
        rhs_scr[:, h * QB:(h + 1) * QB] = jnp.concatenate([qh[s], ql[s], qh[s], zi], axis=0)

    rowi = lax.broadcasted_iota(jnp.int32, (LANES, QB), 0)
    coli = lax.broadcasted_iota(jnp.int32, (LANES, QB), 1)
    q_chunk = (j * QB + coli) // CHUNK

    def score_body(c, carry):
        off = pl.multiple_of(c * LANES, LANES)
        s_all = jnp.dot(ki3_ref[pl.ds(off, LANES), :], rhs_scr[...], preferred_element_type=F32)
        acc = jnp.zeros((LANES, QB), F32)
        for h in range(IDX_HEADS):
            acc = acc + jnp.maximum(s_all[:, h * QB:(h + 1) * QB], 0.0) * wiT[h:h + 1, :]
        adm = (c * LANES + rowi) // CHUNK <= q_chunk
        acc = jnp.where(adm, acc, -jnp.inf)
        bits = pltpu.bitcast(acc, jnp.int32)
        key_scr[pl.ds(off, LANES), :] = jnp.where(bits < 0, bits ^ 0x7FFFFFFF, bits)
        return carry

    lax.fori_loop(0, nk, score_body, 0)

    def count(pred):
        def body(c, cnt):
            off = pl.multiple_of(c * LANES, LANES)
            hit = pred(key_scr[pl.ds(off, LANES), :])
            return cnt + jnp.sum(jnp.where(hit, jnp.int32(1), jnp.int32(0)), axis=0, keepdims=True, dtype=jnp.int32)
        return lax.fori_loop(0, nk, body, jnp.zeros((1, QB), jnp.int32))

    prefix = jnp.where(count(lambda kb: kb >= 0) >= top_k, 0, INT_MIN).astype(jnp.int32)

    def bit_body(i, prefix):
        cand = prefix | (jnp.int32(1) << (30 - i))
        return jnp.where(count(lambda kb: kb >= cand) >= top_k, cand, prefix)

    thr = lax.fori_loop(0, 31, bit_body, prefix)
    need = (top_k - count(lambda kb: kb > thr)).astype(F32)

    m_scr[...] = jnp.full_like(m_scr, NEG_BIG)
    l_scr[...] = jnp.zeros_like(l_scr)
    acc_scr[...] = jnp.zeros_like(acc_scr)
    tri = (lax.broadcasted_iota(jnp.int32, (LANES, LANES), 1)
           < lax.broadcasted_iota(jnp.int32, (LANES, LANES), 0)).astype(BF16)

    def attn_body(c, eq_seen):
        off = pl.multiple_of(c * LANES, LANES)
        kb = key_scr[pl.ds(off, LANES), :]
        eq = kb == thr
        eqf = jnp.where(eq, 1.0, 0.0)
        before = eq_seen + jnp.dot(tri, eqf.astype(BF16), preferred_element_type=F32)
        take = eq & (before < need)
        sel = (kb > KEY_NEG_INF) & ((kb > thr) | take)
        kvc = kv_ref[pl.ds(off, LANES), :]
        vtc = vt_ref[c]
        for h in range(H):
            s = jnp.dot(kvc, qt_scr[h], preferred_element_type=F32)
            s = jnp.where(sel, s, NEG_BIG)
            m_old = m_scr[h:h + 1, :]
            m_new = jnp.maximum(m_old, jnp.max(s, axis=0, keepdims=True))
            alpha = jnp.exp(m_old - m_new)
            p = jnp.where(sel, jnp.exp(s - m_new), 0.0)
            l_scr[h:h + 1, :] = alpha * l_scr[h:h + 1, :] + jnp.sum(p, axis=0, keepdims=True)
            acc_scr[h] = acc_scr[h] * alpha + jnp.dot(vtc, p.astype(BF16), preferred_element_type=F32)
            m_scr[h:h + 1, :] = m_new
        return eq_seen + jnp.sum(eqf, axis=0, keepdims=True)

    lax.fori_loop(0, nk, attn_body, jnp.zeros((1, QB), F32))

    outT = jnp.concatenate([acc_scr[h] / l_scr[h:h + 1, :] for h in range(H)], axis=0)
    o_ref[...] = outT.T * _silu(gb_ref[...])


def _dsa(qr, qiw, p_main, gb_col, kv, vt, ki3, B, T, top_k):
    nq = T // Q_BLOCK
    m = B * T
    vt = vt.reshape(B, T // LANES, B_HEAD_DIM, LANES)
    qblk = lambda w, ci=0: pl.BlockSpec((Q_BLOCK, w), lambda b, j, ci=ci: (b * nq + j, ci))
    return pl.pallas_call(
        functools.partial(_dsa_kernel, top_k),
        grid=(B, nq),
        in_specs=[qblk(B_WIDTH), qblk(3 * LANES), qblk(B_WIDTH, gb_col),
                  pl.BlockSpec((T, LANES), lambda b, j: (b, 0)),
                  pl.BlockSpec((None, T // LANES, B_HEAD_DIM, LANES), lambda b, j: (b, 0, 0, 0)),
                  pl.BlockSpec((T, LANES), lambda b, j: (b, 0))],
        out_specs=pl.BlockSpec((Q_BLOCK, B_WIDTH), lambda b, j: (b * nq + j, 0)),
        out_shape=jax.ShapeDtypeStruct((m, B_WIDTH), F32),
        scratch_shapes=[pltpu.VMEM((B_HEADS, LANES, Q_BLOCK), BF16),
                        pltpu.VMEM((LANES, IDX_HEADS * Q_BLOCK), BF16),
                        pltpu.VMEM((T, Q_BLOCK), jnp.int32),
                        pltpu.VMEM((B_HEADS, Q_BLOCK), F32),
                        pltpu.VMEM((B_HEADS, Q_BLOCK), F32),
                        pltpu.VMEM((B_HEADS, B_HEAD_DIM, Q_BLOCK), F32)],
        compiler_params=_cparams(2),
        name="dsa_attn",
    )(qr, qiw, p_main, kv, vt, ki3)


def _gla_kernel(q_ref, k_ref, v_ref, gd_ref, gc_ref, g2_ref, gbias_ref, ng_ref, o_ref, st_ref):
    c = pl.program_id(1)
    L = q_ref.shape[0]
    H, Dk, Dv = C_HEADS, C_KEY_DIM, C_VAL_DIM

    @pl.when(c == 0)
    def _():
        st_ref[...] = jnp.zeros_like(st_ref)

    z = _dotx(gd_ref[...], g2_ref[...]) + gbias_ref[...]
    log_a = -_softplus(-z) / C_GATE_NORM
    ti = lax.broadcasted_iota(jnp.int32, (L, L), 0)
    si = lax.broadcasted_iota(jnp.int32, (L, L), 1)
    incl = si <= ti
    b = _dotx_exact_lhs(incl.astype(BF16), log_a)
    b_last = b[L - 1:L, :]
    q_e = q_ref[...] * (Dk ** -0.5) * jnp.exp(b)
    k = k_ref[...]
    k_e = k * jnp.exp(-b)
    k_dec = k * jnp.exp(b_last - b)
    dec = jnp.exp(b_last)
    v = v_ref[...]
    ng = ng_ref[...]
    outs = []
    for h in range(H):
        ks = slice(h * Dk, (h + 1) * Dk)
        v_h = v[:, h * Dv:(h + 1) * Dv]
        att = jnp.where(incl, _dot_nt(q_e[:, ks], k_e[:, ks]), 0.0)
        S = st_ref[h]
        o = _dot(att, v_h) + _dot(q_e[:, ks], S)
        dcol = jnp.broadcast_to(dec[:, ks], (Dk, Dk)).T
        st_ref[h] = S * jnp.concatenate([dcol] * (Dv // Dk), axis=1) + _dot_tn(k_dec[:, ks], v_h)
        o = o * lax.rsqrt(jnp.mean(o * o, axis=-1, keepdims=True) + RMS_EPS) * ng
        outs.append(o)
    o_ref[...] = jnp.concatenate(outs, axis=1) * _silu(gc_ref[...])


def _gla(p, cols, B, T, g2p, gbias, ng, L=CHUNK):
    nc = T // L
    m = B * T
    blk = lambda w, ci: pl.BlockSpec((L, w), lambda b, c, ci=ci: (b * nc + c, ci))
    full = lambda arr: pl.BlockSpec(arr.shape, lambda b, c: (0,) * arr.ndim)
    return pl.pallas_call(
        _gla_kernel,
        grid=(B, nc),
        in_specs=[blk(C_KEY_WIDTH, cols["q"]), blk(C_KEY_WIDTH, cols["k"]), blk(C_VAL_WIDTH, cols["v"]),
                  blk(LANES, cols["gd"]), blk(C_VAL_WIDTH, cols["gc"]),
                  full(g2p), full(gbias), full(ng)],
        out_specs=pl.BlockSpec((L, C_VAL_WIDTH), lambda b, c: (b * nc + c, 0)),
        out_shape=jax.ShapeDtypeStruct((m, C_VAL_WIDTH), F32),
        scratch_shapes=[pltpu.VMEM((C_HEADS, C_KEY_DIM, C_VAL_DIM), F32)],
        compiler_params=_cparams(2),
        name="gla",
    )(p, p, p, p, p, g2p, gbias, ng)


def _out_ln_kernel(n_in, *refs):
    y_refs = refs[:n_in]
    w_refs = refs[n_in:2 * n_in]
    h_ref, g_ref, b_ref, o_ref = refs[2 * n_in:]
    y = None
    for y_ref, w_ref in zip(y_refs, w_refs):
        t = jnp.dot(y_ref[...].astype(BF16), w_ref[...], preferred_element_type=F32)
        y = t if y is None else y + t
    z = DN_ALPHA * h_ref[...] + y
    mu = jnp.mean(z, axis=-1, keepdims=True)
    zc = z - mu
    var = jnp.mean(zc * zc, axis=-1, keepdims=True)
    o_ref[...] = zc * lax.rsqrt(var + LN_EPS) * g_ref[...] + b_ref[...]


def _out_ln(ys, ws, h2d, g, b, tm=256):
    m, d = h2d.shape
    n_in = len(ys)
    row = lambda arr: pl.BlockSpec((tm, arr.shape[1]), lambda i: (i, 0))
    full = lambda arr: pl.BlockSpec(arr.shape, lambda i: (0,) * arr.ndim)
    return pl.pallas_call(
        functools.partial(_out_ln_kernel, n_in),
        grid=(m // tm,),
        in_specs=[row(y) for y in ys] + [full(w) for w in ws] + [row(h2d), full(g), full(b)],
        out_specs=pl.BlockSpec((tm, d), lambda i: (i, 0)),
        out_shape=jax.ShapeDtypeStruct((m, d), F32),
        compiler_params=_cparams(1),
        name="out_ln",
    )(*ys, *ws, h2d, g, b)


def _rope_tables(positions, head_dim):
    rd = head_dim // 4
    half = rd // 2
    inv = jnp.power(ROPE_THETA, -jnp.arange(half, dtype=F32) * 2.0 / rd)
    ang = positions.astype(F32).reshape(-1, 1) * inv
    cos, sin = jnp.cos(ang), jnp.sin(ang)
    lane = np.arange(LANES) % head_dim
    idx = np.where(lane < rd, lane % half, 0)
    cos_l, sin_l = cos[:, idx], sin[:, idx]
    c = jnp.where(lane < rd, cos_l, 1.0)
    sa = jnp.where(lane < half, -sin_l, 0.0)
    sb = jnp.where((lane >= half) & (lane < rd), sin_l, 0.0)
    return c, sa, sb


def _pad_cols(w, n):
    return jnp.pad(w, ((0, 0), (0, n - w.shape[1])))


def _pad_rows(w, n, at=0):
    return jnp.pad(w, ((at, n - w.shape[0] - at), (0, 0)))


def _even_layer(h2d, positions, B, T, e_w_in, a_mu, a_w0, a_w2, a_a0, a_a2, a_kk_scale, a_ka, a_rk,
                a_gn_g, a_gn_b, b_kv_norm_g, b_wuk, b_wuv, e_w_out, ln_g, ln_b):
    W = A_WIDTH
    o = np.cumsum([0, W, W, W, A_RANK, A_RANK, W, B_WIDTH, B_KV_RANK, IDX_HEADS * IDX_DIM, IDX_DIM, IDX_HEADS, B_WIDTH])
    w_r, w_k, w_v, w_wd, w_ad, w_ga, w_qb, w_ckv, w_qi, w_ki, w_wi, w_gb = (
        e_w_in[:, o[i]:o[i + 1]] for i in range(12))
    w_main = jnp.concatenate([w_r, w_k, w_v, w_ga, w_qb, w_gb, w_wd, w_ad, w_ckv], axis=1).astype(BF16)
    cols = dict(r=0, k=1, v=2, ga=3, qb=4, gb=5, wa=(6 * W) // LANES, ckv=(6 * W) // LANES + 1)
    w_idx = jnp.concatenate([w_qi, _pad_cols(w_wi, LANES), _pad_cols(jnp.concatenate([w_ki] * 3, axis=1), LANES)], axis=1)
    wih, wil = _hi_lo(w_idx)
    p_main, p_idx = _project_with_idx(h2d, w_main, wih, wil)

    row = lambda t: t.reshape(1, -1)
    pa = jnp.concatenate([row(a_mu[:W]), row(a_mu[W:2 * W]), row(a_mu[2 * W:3 * W]), row(a_w0), row(a_a0),
                          row(a_kk_scale), row(a_ka), row(a_rk), row(a_gn_g), row(a_gn_b),
                          jnp.zeros((6, W), F32)], axis=0)
    muwa = row(a_mu[3 * W:])
    w2p = _pad_rows(a_w2, LANES)
    a2p = _pad_rows(a_a2, LANES, at=A_RANK)
    hid = np.arange(W) // A_HEAD_DIM
    bd = jnp.asarray(hid[:, None] == hid[None, :], BF16)
    ya = _rwkv(p_main, cols, B, T, pa, muwa, w2p, a2p, bd)

    tabs = _rope_tables(positions, B_HEAD_DIM) + _rope_tables(positions, IDX_DIM)
    wkv = jnp.concatenate([b_wuk, b_wuv], axis=1).astype(BF16)
    qr, kv, vt, qiw, ki3 = _dsa_prep(p_main, p_idx, cols, tabs, row(b_kv_norm_g), wkv, B, T)
    yb = _dsa(qr, qiw, p_main, cols["gb"], kv, vt, ki3, B, T, min(TOPK_MAX, T // 4))

    w_out = e_w_out.astype(BF16)
    return _out_ln([ya, yb], [w_out[:W], w_out[W:]], h2d, row(ln_g), row(ln_b))


def _odd_layer(h2d, B, T, o_w_in, c_g2, c_g_bias, c_norm_g, o_w_out, ln_g, ln_b):
    o = np.cumsum([0, C_KEY_WIDTH, C_KEY_WIDTH, C_VAL_WIDTH, C_GATE_RANK, C_VAL_WIDTH])
    w_q, w_k, w_v, w_gd, w_gc = (o_w_in[:, o[i]:o[i + 1]] for i in range(5))
    w = jnp.concatenate([w_q, w_k, w_v, w_gc, _pad_cols(w_gd, LANES)], axis=1).astype(BF16)
    cols = dict(q=0, k=1, v=1, gc=2, gd=(2 * C_KEY_WIDTH + 2 * C_VAL_WIDTH) // LANES)
    p = _project(h2d, w)
    row = lambda t: t.reshape(1, -1)
    yc = _gla(p, cols, B, T, _pad_rows(c_g2, LANES), row(c_g_bias), row(c_norm_g))
    return _out_ln([yc], [o_w_out.astype(BF16)], h2d, row(ln_g), row(ln_b))


def kernel(x, positions, e_w_in, a_mu, a_w0, a_w2, a_a0, a_a2, a_kk_scale, a_ka, a_rk, a_gn_g, a_gn_b, b_kv_norm_g, b_wuk, b_wuv, e_w_out, o_w_in, c_g2, c_g_bias, c_norm_g, o_w_out, ln_g, ln_b):
    B, T, D = x.shape
    h = x.reshape(B * T, D)
    for i in range(DEPTH):
        j = i // 2
        if i % 2 == 0:
            h = _even_layer(h, positions, B, T, e_w_in[j], a_mu[j], a_w0[j], a_w2[j], a_a0[j], a_a2[j],
                            a_kk_scale[j], a_ka[j], a_rk[j].reshape(-1), a_gn_g[j], a_gn_b[j],
                            b_kv_norm_g[j], b_wuk[j], b_wuv[j], e_w_out[j], ln_g[i], ln_b[i])
        else:
            h = _odd_layer(h, B, T, o_w_in[j], c_g2[j], c_g_bias[j], c_norm_g[j], o_w_out[j], ln_g[i], ln_b[i])
    return h.reshape(B, T, D)
```
